```python
import math
import jax
import jax.numpy as jnp
from jax import lax
import numpy as np

D_MODEL = 2048
BATCH = 8
SEQ = 2048
DEPTH = 1
DEC_BATCH = 1
DEC_SEQ = 8192
PAST_LEN = 128

GRID_W = 64
DA_HEADS = 8
DA_DK = 64
DA_DV = 2 * DA_DK
DA_QBLK = 128
DA_W = DA_HEADS * DA_DV
NA_HEADS = 8
NA_DH = 128
NA_KH = 8
NA_KW = 16
NA_QBW = NA_KW
NA_KBW = 2 * NA_KW
NA_W = NA_HEADS * NA_DH
T5_BUCKETS = 32
T5_MAX_DIST = 128
N_GROUPS = 4
EXPERTS_PER_GROUP = 8
N_EXPERTS = N_GROUPS * EXPERTS_PER_GROUP
EXPERT_FF = 512
TOP_K = 2
MOE_BLOCK = 128

IN_WIDTH = 2 * (DA_HEADS * 2 * DA_DK) + DA_W + 3 * NA_W + 2 * D_MODEL
EPS = 1e-6
NEG_INF = -1e30

kernel_name = 'hybrid_diffattn_natten_hmoe_encoder'

F32 = jnp.float32


def rmsnorm(x, g):
    x32 = x.astype(F32)
    y = x32 * lax.rsqrt(jnp.mean(x32 * x32, axis=-1, keepdims=True) + EPS)
    return (y * g.astype(F32)).astype(x.dtype)


def t5_bucket(rel):
    half = T5_BUCKETS // 2
    max_exact = half // 2
    n = jnp.abs(rel)
    n_f = jnp.maximum(n, 1).astype(F32)
    large = max_exact + (jnp.log(n_f / max_exact) / math.log(T5_MAX_DIST / max_exact) * (half - max_exact)).astype(jnp.int32)
    large = jnp.minimum(large, half - 1)
    return jnp.where(rel > 0, half, 0) + jnp.where(n < max_exact, n, large)


def diff_attention(q, k, v, lam, rel_bias):
    B, S = q.shape[0], q.shape[1]
    nqb = S // DA_QBLK
    scale = DA_DK ** -0.5
    qb = q.reshape(B, nqb, DA_QBLK, DA_HEADS, 2, DA_DK).swapaxes(0, 1)
    kpos = jnp.arange(S, dtype=jnp.int32)

    def block(args):
        qblk, i = args
        qpos = i * DA_QBLK + jnp.arange(DA_QBLK, dtype=jnp.int32)
        bias = rel_bias[t5_bucket(kpos[None, :] - qpos[:, None])].astype(F32)
        bias = jnp.transpose(bias, (2, 0, 1))
        s = jnp.einsum('bqhcd,bkhcd->bchqk', qblk, k).astype(F32) * scale + bias
        p = jax.nn.softmax(s, axis=-1)
        a = p[:, 0] - lam * p[:, 1]
        return jnp.einsum('bhqk,bkhd->bqhd', a.astype(v.dtype), v)

    o = lax.map(block, (qb, jnp.arange(nqb, dtype=jnp.int32)))
    return o.swapaxes(0, 1).reshape(B, S, DA_HEADS, DA_DV)


def neighborhood_attention(q, k, v, rpb):
    B, S, H, DH = q.shape
    rows = S // GRID_W
    kh = min(NA_KH, rows)
    nj = GRID_W // NA_QBW
    scale = DH ** -0.5
    r_idx = jnp.arange(rows, dtype=jnp.int32)
    row_start = jnp.clip(r_idx - kh // 2, 0, rows - kh)
    j_idx = jnp.arange(nj, dtype=jnp.int32)
    blk_col = jnp.clip(j_idx * NA_QBW - NA_KW // 2, 0, GRID_W - NA_KBW)
    kcol = blk_col[:, None] + jnp.arange(NA_KBW, dtype=jnp.int32)
    qcol = j_idx[:, None] * NA_QBW + jnp.arange(NA_QBW, dtype=jnp.int32)
    win_col = jnp.clip(qcol - NA_KW // 2, 0, GRID_W - NA_KW)
    in_win = (kcol[:, None, :] >= win_col[:, :, None]) & (kcol[:, None, :] < win_col[:, :, None] + NA_KW)
    dc_idx = jnp.clip(kcol[:, None, :] - qcol[:, :, None] + NA_KW - 1, 0, 2 * NA_KW - 2)
    krow = row_start[:, None] + jnp.arange(kh, dtype=jnp.int32)
    kidx = krow[:, None, :, None] * GRID_W + kcol[None, :, None, :]
    qg = q.reshape(B, rows, nj, NA_QBW, H, DH).swapaxes(0, 1)

    def row_block(args):
        qrow, kidx_r, krow_r, r = args
        kb = k[:, kidx_r]
        vb = v[:, kidx_r]
        dr_idx = krow_r - r + NA_KH - 1
        bias = rpb[:, dr_idx[None, None, :, None], dc_idx[:, :, None, :]].astype(F32)
        s = jnp.einsum('bjqhd,bjrkhd->bhjqrk', qrow, kb).astype(F32) * scale + bias
        s = jnp.where(in_win[:, :, None, :], s, NEG_INF)
        shp = s.shape
        p = jax.nn.softmax(s.reshape(shp[:4] + (kh * NA_KBW,)), axis=-1).reshape(shp)
        return jnp.einsum('bhjqrk,bjrkhd->bjqhd', p.astype(v.dtype), vb)

    o = lax.map(row_block, (qg, kidx, krow, r_idx))
    return o.swapaxes(0, 1).reshape(B, S, H * DH)


def token_mixer(xn, w_in, lam_q1, lam_k1, lam_q2, lam_k2, subln_g, na_rpb, w_a_proj, w_b_proj, w_o, rel_bias, lam_init):
    B, S, _ = xn.shape
    z = jnp.dot(xn, w_in)
    widths = [DA_HEADS * 2 * DA_DK, DA_HEADS * 2 * DA_DK, DA_W, NA_W, NA_W, NA_W, D_MODEL, D_MODEL]
    parts = []
    off = 0
    for w in widths:
        parts.append(z[..., off:off + w])
        off += w
    qa, ka, va, qb, kb, vb, ga, gb = parts
    lam = (jnp.exp(jnp.sum(lam_q1.astype(F32) * lam_k1.astype(F32)))
           - jnp.exp(jnp.sum(lam_q2.astype(F32) * lam_k2.astype(F32))) + lam_init)
    oa = diff_attention(qa.reshape(B, S, DA_HEADS, 2, DA_DK), ka.reshape(B, S, DA_HEADS, 2, DA_DK),
                        va.reshape(B, S, DA_HEADS, DA_DV), lam, rel_bias)
    oa = (rmsnorm(oa, subln_g) * (1.0 - lam_init)).reshape(B, S, DA_W)
    ob = neighborhood_attention(qb.reshape(B, S, NA_HEADS, NA_DH), kb.reshape(B, S, NA_HEADS, NA_DH),
                                vb.reshape(B, S, NA_HEADS, NA_DH), na_rpb)
    merged = jax.nn.sigmoid(ga) * jnp.dot(oa, w_a_proj) + jax.nn.sigmoid(gb) * jnp.dot(ob, w_b_proj)
    return jnp.dot(merged, w_o)


def hier_moe(xt, w_rg, b_rg, w_re, b_re, w1, w3, w2):
    T, D = xt.shape
    g_logits = jnp.dot(xt, w_rg).astype(F32) + b_rg.astype(F32)
    g_prob, g_idx = lax.top_k(jax.nn.softmax(g_logits, axis=-1), 1)
    e_logits = jnp.einsum('td,gde->tge', xt, w_re).astype(F32) + b_re.astype(F32)
    e_logits = jnp.take_along_axis(e_logits, g_idx[:, :, None], axis=1)[:, 0]
    top_v, top_i = lax.top_k(e_logits, TOP_K)
    gate = jax.nn.softmax(top_v, axis=-1) * g_prob
    expert = g_idx * EXPERTS_PER_GROUP + top_i
    A = T * TOP_K
    e_flat = expert.reshape(A)
    g_flat = gate.reshape(A)
    tok = jnp.arange(A, dtype=jnp.int32) // TOP_K
    order = jnp.argsort(e_flat)
    e_s = e_flat[order]
    tok_s = tok[order]
    g_s = g_flat[order]
    counts = jnp.bincount(e_flat, length=N_EXPERTS)
    start = jnp.cumsum(counts) - counts
    padded = (counts + MOE_BLOCK - 1) // MOE_BLOCK * MOE_BLOCK
    pend = jnp.cumsum(padded)
    pstart = pend - padded
    dest = pstart[e_s] + jnp.arange(A, dtype=jnp.int32) - start[e_s]
    n_blocks = -(-A // MOE_BLOCK) + N_EXPERTS
    P = n_blocks * MOE_BLOCK
    slot_tok = jnp.full((P,), T, dtype=jnp.int32).at[dest].set(tok_s)
    xpad = jnp.concatenate([xt, jnp.zeros((1, D), xt.dtype)], axis=0)
    xb = xpad[slot_tok].reshape(n_blocks, MOE_BLOCK, D)
    block_e = jnp.minimum(jnp.searchsorted(pend, jnp.arange(n_blocks, dtype=jnp.int32) * MOE_BLOCK, side='right'),
                          N_EXPERTS - 1)

    def run(args):
        xblk, e = args
        h = jax.nn.silu(jnp.dot(xblk, w1[e])) * jnp.dot(xblk, w3[e])
        return jnp.dot(h, w2[e])

    yb = lax.map(run, (xb, block_e)).reshape(P, D)
    return jnp.zeros((T, D), xt.dtype).at[tok_s].add(yb[dest] * g_s[:, None].astype(xt.dtype))


def encode(x, rel_bias, norm1, w_in, lam_q1, lam_k1, lam_q2, lam_k2, subln_g, na_rpb, w_a_proj, w_b_proj, w_o,
           norm2, w_route_group, b_route_group, w_route_expert, b_route_expert, w1, w3, w2, final_norm):
    for l in range(DEPTH):
        lam_init = 0.8 - 0.6 * math.exp(-0.3 * l)
        h = x + token_mixer(rmsnorm(x, norm1[l]), w_in[l], lam_q1[l], lam_k1[l], lam_q2[l], lam_k2[l], subln_g[l],
                            na_rpb[l], w_a_proj[l], w_b_proj[l], w_o[l], rel_bias, lam_init)
        B, S, D = h.shape
        m = hier_moe(rmsnorm(h, norm2[l]).reshape(B * S, D), w_route_group[l], b_route_group[l], w_route_expert[l],
                     b_route_expert[l], w1[l], w3[l], w2[l]).reshape(B, S, D)
        x = h + m
    return rmsnorm(x, final_norm)


def setup_inputs(seed: int = 0) -> dict:
    key = jax.random.key(seed)
    ks = jax.random.split(key, 23)

    def nrm(k, shape, s):
        return jax.random.normal(k, shape, F32) * s

    D = D_MODEL
    return {
        'x_prompt': nrm(ks[0], (BATCH, SEQ, D), 1.0),
        'x_sample': nrm(ks[1], (DEC_BATCH, DEC_SEQ, D), 1.0),
        'rel_bias': nrm(ks[2], (T5_BUCKETS, DA_HEADS), 0.5),
        'norm1': 1.0 + nrm(ks[3], (DEPTH, D), 0.02),
        'w_in': nrm(ks[4], (DEPTH, D, IN_WIDTH), D ** -0.5),
        'lam_q1': nrm(ks[5], (DEPTH, DA_DK), 0.1),
        'lam_k1': nrm(ks[6], (DEPTH, DA_DK), 0.1),
        'lam_q2': nrm(ks[7], (DEPTH, DA_DK), 0.1),
        'lam_k2': nrm(ks[8], (DEPTH, DA_DK), 0.1),
        'subln_g': 1.0 + nrm(ks[9], (DEPTH, DA_DV), 0.02),
        'na_rpb': nrm(ks[10], (DEPTH, NA_HEADS, 2 * NA_KH - 1, 2 * NA_KW - 1), 0.5),
        'w_a_proj': nrm(ks[11], (DEPTH, DA_W, D), DA_W ** -0.5),
        'w_b_proj': nrm(ks[12], (DEPTH, NA_W, D), NA_W ** -0.5),
        'w_o': nrm(ks[13], (DEPTH, D, D), D ** -0.5),
        'norm2': 1.0 + nrm(ks[14], (DEPTH, D), 0.02),
        'w_route_group': nrm(ks[15], (DEPTH, D, N_GROUPS), D ** -0.5),
        'b_route_group': nrm(ks[16], (DEPTH, N_GROUPS), 0.01),
        'w_route_expert': nrm(ks[17], (DEPTH, N_GROUPS, D, EXPERTS_PER_GROUP), D ** -0.5),
        'b_route_expert': nrm(ks[18], (DEPTH, N_GROUPS, EXPERTS_PER_GROUP), 0.01),
        'w1': nrm(ks[19], (DEPTH, N_EXPERTS, D, EXPERT_FF), D ** -0.5),
        'w3': nrm(ks[20], (DEPTH, N_EXPERTS, D, EXPERT_FF), D ** -0.5),
        'w2': nrm(ks[21], (DEPTH, N_EXPERTS, EXPERT_FF, D), EXPERT_FF ** -0.5),
        'final_norm': 1.0 + nrm(ks[22], (D,), 0.02),
    }


def reference(x_prompt, x_sample, rel_bias, norm1, w_in, lam_q1, lam_k1, lam_q2, lam_k2, subln_g, na_rpb, w_a_proj,
              w_b_proj, w_o, norm2, w_route_group, b_route_group, w_route_expert, b_route_expert, w1, w3, w2,
              final_norm):
    y_prompt = encode(x_prompt, rel_bias, norm1, w_in, lam_q1, lam_k1, lam_q2, lam_k2, subln_g, na_rpb, w_a_proj,
                      w_b_proj, w_o, norm2, w_route_group, b_route_group, w_route_expert, b_route_expert, w1, w3, w2,
                      final_norm)
    y_sample = encode(x_sample, rel_bias, norm1, w_in, lam_q1, lam_k1, lam_q2, lam_k2, subln_g, na_rpb, w_a_proj,
                      w_b_proj, w_o, norm2, w_route_group, b_route_group, w_route_expert, b_route_expert, w1, w3, w2,
                      final_norm)
    return (y_prompt, y_sample)
```

```python
import functools
import math

import jax
import jax.numpy as jnp
from jax import lax
from jax.experimental import pallas as pl
from jax.experimental.pallas import tpu as pltpu

F32 = jnp.float32
BF16 = jnp.bfloat16

D_MODEL = 2048
GRID_W = 64
DA_HEADS = 8
DA_DK = 64
DA_DV = 2 * DA_DK
DA_W = DA_HEADS * DA_DV
NA_HEADS = 8
NA_DH = 128
NA_KH = 8
NA_KW = 16
NA_W = NA_HEADS * NA_DH
T5_BUCKETS = 32
T5_MAX_DIST = 128
N_GROUPS = 4
EXPERTS_PER_GROUP = 8
N_EXPERTS = N_GROUPS * EXPERTS_PER_GROUP
EXPERT_FF = 512
TOP_K = 2
IN_WIDTH = 2 * (DA_HEADS * 2 * DA_DK) + DA_W + 3 * NA_W + 2 * D_MODEL
EPS = 1e-6
NEG_INF = -1e30
LAM_INIT = 0.8 - 0.6 * math.exp(-0.3 * 0)

LANE = 128
HEAD_W = 128
ROW_TILES = D_MODEL // LANE

QA_BLK = 0
KA_BLK = DA_HEADS
VA_BLK = 2 * DA_HEADS
QB_BLK = 3 * DA_HEADS
KB_BLK = QB_BLK + NA_HEADS
VB_BLK = KB_BLK + NA_HEADS
GATE_A_BLK = (VB_BLK + NA_HEADS) * HEAD_W // D_MODEL
GATE_B_BLK = GATE_A_BLK + 1

PROJ_TM = 1024
PROJ_TN = 1024
DA_TQ = 256
DA_TK = 512
NA_QROWS = 4
NA_QB = NA_QROWS * GRID_W
NA_KWIN = 3 * NA_QB
MERGE_TM = 256
MOE_BM = 256
COMB_TM = 256
ROUTE_W = 128

VMEM_LIMIT = 56 * 1024 * 1024


def _params(sem, vmem=VMEM_LIMIT):
    return pltpu.CompilerParams(dimension_semantics=sem, vmem_limit_bytes=vmem)


def _proj_kernel(x_ref, g_ref, w_ref, o_ref, xn_ref):
    @pl.when(pl.program_id(1) == 0)
    def _():
        x = x_ref[...]
        ms = jnp.mean(x * x, axis=-1, keepdims=True)
        xn_ref[...] = (x * lax.rsqrt(ms + EPS) * g_ref[...]).astype(BF16)

    o_ref[...] = jnp.dot(xn_ref[...], w_ref[...], preferred_element_type=F32).astype(o_ref.dtype)


def _proj(x2, g, w_bf):
    T, D = x2.shape
    N = w_bf.shape[1]
    tm = min(PROJ_TM, T)
    return pl.pallas_call(
        _proj_kernel,
        grid=(T // tm, N // PROJ_TN),
        in_specs=[
            pl.BlockSpec((tm, D), lambda i, j: (i, 0)),
            pl.BlockSpec((1, D), lambda i, j: (0, 0)),
            pl.BlockSpec((D, PROJ_TN), lambda i, j: (0, j)),
        ],
        out_specs=pl.BlockSpec((tm, PROJ_TN), lambda i, j: (i, j)),
        out_shape=jax.ShapeDtypeStruct((T, N), BF16),
        scratch_shapes=[pltpu.VMEM((tm, D), BF16)],
        compiler_params=_params(("arbitrary", "arbitrary")),
        name="proj",
    )(x2, g, w_bf)


def _t5_bucket(rel):
    half = T5_BUCKETS // 2
    max_exact = half // 2
    n = jnp.abs(rel)
    n_f = jnp.maximum(n, 1).astype(F32)
    large = max_exact + (jnp.log(n_f / max_exact) / math.log(T5_MAX_DIST / max_exact) * (half - max_exact)).astype(jnp.int32)
    large = jnp.minimum(large, half - 1)
    return jnp.where(rel > 0, half, 0) + jnp.where(n < max_exact, n, large)


def _da_pattern_range():
    lo = -((DA_TK + T5_MAX_DIST - 1) // DA_TQ + 1)
    hi = (DA_TQ + T5_MAX_DIST - 1) // DA_TQ + 1
    return lo, hi - lo + 1


def _da_bias_patterns(rel_bias):
    lo, npat = _da_pattern_range()
    d = (jnp.arange(npat, dtype=jnp.int32) + lo) * DA_TQ
    kk = jnp.arange(DA_TK, dtype=jnp.int32)
    qq = jnp.arange(DA_TQ, dtype=jnp.int32)
    rel = d[:, None, None] + kk[None, :, None] - qq[None, None, :]
    bucket = _t5_bucket(rel)
    return jnp.transpose(rel_bias.astype(F32)[bucket], (3, 0, 1, 2))


def _da_kernel(lq1_ref, lk1_ref, lq2_ref, lk2_ref, q_ref, k_ref, v_ref, bias_ref, g_ref, o_ref,
               vt_ref, acc_ref, m_ref, l_ref, *, nk, pat_lo, npat):
    qi = pl.program_id(2)

    @pl.when(qi == 0)
    def _():
        for kt in range(nk):
            vt_ref[kt] = v_ref[kt * DA_TK:(kt + 1) * DA_TK, :].astype(F32).T.astype(BF16)

    qs = (q_ref[...].astype(F32) * (DA_DK ** -0.5)).astype(BF16)
    lane = lax.broadcasted_iota(jnp.int32, qs.shape, 1)
    zero = jnp.zeros_like(qs)
    q_halves = (jnp.where(lane < DA_DK, qs, zero), jnp.where(lane >= DA_DK, qs, zero))

    m_ref[...] = jnp.full(m_ref.shape, NEG_INF, F32)
    l_ref[...] = jnp.zeros(l_ref.shape, F32)
    acc_ref[...] = jnp.zeros(acc_ref.shape, F32)
    ratio = DA_TK // DA_TQ

    def body(kt, carry):
        k = k_ref[pl.ds(pl.multiple_of(kt * DA_TK, DA_TK), DA_TK), :]
        vt = vt_ref[kt]
        pat = jnp.clip(kt * ratio - qi - pat_lo, 0, npat - 1)
        bias = bias_ref[0, pat]
        for c in range(2):
            s = lax.dot_general(k, q_halves[c], (((1,), (1,)), ((), ())), preferred_element_type=F32) + bias
            m_old = m_ref[c]
            m_new = jnp.maximum(m_old, jnp.max(s, axis=0, keepdims=True))
            alpha = jnp.exp(m_old - m_new)
            p = jnp.exp(s - m_new)
            l_ref[c] = alpha * l_ref[c] + jnp.sum(p, axis=0, keepdims=True)
            acc_ref[c] = alpha * acc_ref[c] + jnp.dot(vt, p.astype(BF16), preferred_element_type=F32)
            m_ref[c] = m_new
        return carry

    lax.fori_loop(0, nk, body, 0)

    lam = (jnp.exp(jnp.sum(lq1_ref[...] * lk1_ref[...], keepdims=True))
           - jnp.exp(jnp.sum(lq2_ref[...] * lk2_ref[...], keepdims=True)) + LAM_INIT)
    o = acc_ref[0] / l_ref[0] - lam * (acc_ref[1] / l_ref[1])
    ms = jnp.mean(o * o, axis=0, keepdims=True)
    o = o * lax.rsqrt(ms + EPS) * g_ref[...] * (1.0 - LAM_INIT)
    o_ref[...] = o.T.astype(o_ref.dtype)


def _diff_attention(z, da_bias, lams, subln_col, B, S):
    nq = S // DA_TQ
    nk = S // DA_TK
    pat_lo, npat = _da_pattern_range()
    kern = functools.partial(_da_kernel, nk=nk, pat_lo=pat_lo, npat=npat)
    lam_spec = pl.BlockSpec((1, DA_DK), lambda b, h, i: (0, 0))
    return pl.pallas_call(
        kern,
        grid=(B, DA_HEADS, nq),
        in_specs=[
            lam_spec, lam_spec, lam_spec, lam_spec,
            pl.BlockSpec((DA_TQ, HEAD_W), lambda b, h, i: (b * nq + i, QA_BLK + h)),
            pl.BlockSpec((S, HEAD_W), lambda b, h, i: (b, KA_BLK + h)),
            pl.BlockSpec((S, HEAD_W), lambda b, h, i: (b, VA_BLK + h)),
            pl.BlockSpec((1, npat, DA_TK, DA_TQ), lambda b, h, i: (h, 0, 0, 0)),
            pl.BlockSpec((DA_DV, 1), lambda b, h, i: (0, 0)),
        ],
        out_specs=pl.BlockSpec((DA_TQ, HEAD_W), lambda b, h, i: (b * nq + i, h)),
        out_shape=jax.ShapeDtypeStruct((B * S, DA_W), BF16),
        scratch_shapes=[
            pltpu.VMEM((nk, DA_DV, DA_TK), BF16),
            pltpu.VMEM((2, DA_DV, DA_TQ), F32),
            pltpu.VMEM((2, 1, DA_TQ), F32),
            pltpu.VMEM((2, 1, DA_TQ), F32),
        ],
        compiler_params=_params(("arbitrary", "arbitrary", "arbitrary")),
        name="diffattn",
    )(*lams, z, z, z, da_bias, subln_col)


def _na_bias_patterns(na_rpb, rows):
    nblk = rows // NA_QROWS
    blk = jnp.array([0, 1, nblk - 1], dtype=jnp.int32)
    win_row0 = (jnp.clip(blk, 1, nblk - 2) - 1) * NA_QROWS
    qi = jnp.arange(NA_QB, dtype=jnp.int32)
    ki = jnp.arange(NA_KWIN, dtype=jnp.int32)
    qr = blk[:, None] * NA_QROWS + qi[None, :] // GRID_W
    qc = qi % GRID_W
    kr = win_row0[:, None] + ki[None, :] // GRID_W
    kc = ki % GRID_W
    kh = min(NA_KH, rows)
    row_start = jnp.clip(qr - kh // 2, 0, rows - kh)
    row_ok = (kr[:, None, :] >= row_start[:, :, None]) & (kr[:, None, :] < row_start[:, :, None] + kh)
    win_col = jnp.clip(qc - NA_KW // 2, 0, GRID_W - NA_KW)
    col_ok = (kc[None, :] >= win_col[:, None]) & (kc[None, :] < win_col[:, None] + NA_KW)
    dr = jnp.clip(kr[:, None, :] - qr[:, :, None] + NA_KH - 1, 0, 2 * NA_KH - 2)
    dc = jnp.clip(kc[None, :] - qc[:, None] + NA_KW - 1, 0, 2 * NA_KW - 2)
    bias = na_rpb.astype(F32)[:, dr, dc[None]]
    return jnp.where((row_ok & col_ok[None])[None], bias, NEG_INF)


def _na_kernel(q_ref, k_ref, v_ref, bias_ref, o_ref, *, nblk):
    i = pl.program_id(2)
    start = pl.multiple_of((jnp.clip(i, 1, nblk - 2) - 1) * NA_QB, NA_QB)
    pat = jnp.where(i == 0, 0, jnp.where(i == nblk - 1, 2, 1))
    k = k_ref[pl.ds(start, NA_KWIN), :]
    v = v_ref[pl.ds(start, NA_KWIN), :]
    s = lax.dot_general(q_ref[...], k, (((1,), (1,)), ((), ())), preferred_element_type=F32)
    s = s * (NA_DH ** -0.5) + bias_ref[0, pat]
    m = jnp.max(s, axis=-1, keepdims=True)
    p = jnp.exp(s - m)
    l = jnp.sum(p, axis=-1, keepdims=True)
    o = jnp.dot(p.astype(BF16), v, preferred_element_type=F32) / l
    o_ref[...] = o.astype(o_ref.dtype)


def _neighborhood_attention(z, na_bias, B, S):
    nblk = S // NA_QB
    assert nblk >= 3
    kern = functools.partial(_na_kernel, nblk=nblk)
    return pl.pallas_call(
        kern,
        grid=(B, NA_HEADS, nblk),
        in_specs=[
            pl.BlockSpec((NA_QB, HEAD_W), lambda b, h, i: (b * nblk + i, QB_BLK + h)),
            pl.BlockSpec((S, HEAD_W), lambda b, h, i: (b, KB_BLK + h)),
            pl.BlockSpec((S, HEAD_W), lambda b, h, i: (b, VB_BLK + h)),
            pl.BlockSpec((1, 3, NA_QB, NA_KWIN), lambda b, h, i: (h, 0, 0, 0)),
        ],
        out_specs=pl.BlockSpec((NA_QB, HEAD_W), lambda b, h, i: (b * nblk + i, h)),
        out_shape=jax.ShapeDtypeStruct((B * S, NA_W), BF16),
        compiler_params=_params(("arbitrary", "arbitrary", "arbitrary")),
        name="natten",
    )(z, z, z, na_bias)


def _lane_min_index(mask, lane_f):
    return jnp.min(jnp.where(mask, lane_f, float(ROUTE_W)), axis=-1, keepdims=True)


def _merge_kernel(oa_ref, ob_ref, ga_ref, gb_ref, x_ref, wa_ref, wb_ref, wo_ref, g2_ref, wr_ref, br_ref,
                  h_ref, xt_ref, eid_ref, gate_ref):
    a = jnp.dot(oa_ref[...], wa_ref[...], preferred_element_type=F32)
    b = jnp.dot(ob_ref[...], wb_ref[...], preferred_element_type=F32)
    merged = jax.nn.sigmoid(ga_ref[...].astype(F32)) * a + jax.nn.sigmoid(gb_ref[...].astype(F32)) * b
    h = x_ref[...] + jnp.dot(merged.astype(BF16), wo_ref[...], preferred_element_type=F32)
    h_ref[...] = h
    ms = jnp.mean(h * h, axis=-1, keepdims=True)
    xt = h * lax.rsqrt(ms + EPS) * g2_ref[...]
    for j in range(ROW_TILES):
        xt_ref[:, j, :] = xt[:, j * LANE:(j + 1) * LANE]

    logits = jnp.dot(xt.astype(BF16), wr_ref[...], preferred_element_type=F32) + br_ref[...]
    lane = lax.broadcasted_iota(jnp.int32, logits.shape, 1)
    lane_f = lane.astype(F32)
    ninf = -jnp.inf
    gl = jnp.where(lane < N_GROUPS, logits, ninf)
    gmax = jnp.max(gl, axis=-1, keepdims=True)
    g_prob = 1.0 / jnp.sum(jnp.exp(gl - gmax), axis=-1, keepdims=True)
    g_idx = _lane_min_index(gl == gmax, lane_f).astype(jnp.int32)
    lo = N_GROUPS + g_idx * EXPERTS_PER_GROUP
    el = jnp.where((lane >= lo) & (lane < lo + EXPERTS_PER_GROUP), logits, ninf)
    v1 = jnp.max(el, axis=-1, keepdims=True)
    i1 = _lane_min_index(el == v1, lane_f)
    el2 = jnp.where(lane_f == i1, ninf, el)
    v2 = jnp.max(el2, axis=-1, keepdims=True)
    i2 = _lane_min_index(el2 == v2, lane_f)
    t = jnp.exp(v2 - v1)
    gate1 = g_prob / (1.0 + t)
    gate2 = g_prob * t / (1.0 + t)
    e1 = i1.astype(jnp.int32) - N_GROUPS
    e2 = i2.astype(jnp.int32) - N_GROUPS
    eid_ref[...] = jnp.where(lane == 0, e1, jnp.where(lane == 1, e2, 0))
    gate_ref[...] = jnp.where(lane == 0, gate1, jnp.where(lane == 1, gate2, 0.0))


def _merge(oa, ob, z, x2, wa, wb, wo, g2, wr, br):
    T, D = x2.shape
    tm = MERGE_TM
    const = lambda shape: pl.BlockSpec(shape, lambda i: (0,) * len(shape), pipeline_mode=pl.Buffered(1))
    return pl.pallas_call(
        _merge_kernel,
        grid=(T // tm,),
        in_specs=[
            pl.BlockSpec((tm, DA_W), lambda i: (i, 0)),
            pl.BlockSpec((tm, NA_W), lambda i: (i, 0)),
            pl.BlockSpec((tm, D), lambda i: (i, GATE_A_BLK)),
            pl.BlockSpec((tm, D), lambda i: (i, GATE_B_BLK)),
            pl.BlockSpec((tm, D), lambda i: (i, 0)),
            const((DA_W, D)), const((NA_W, D)), const((D, D)),
            const((1, D)), const((D, ROUTE_W)), const((1, ROUTE_W)),
        ],
        out_specs=[
            pl.BlockSpec((tm, D), lambda i: (i, 0)),
            pl.BlockSpec((tm, ROW_TILES, LANE), lambda i: (i, 0, 0)),
            pl.BlockSpec((tm, ROUTE_W), lambda i: (i, 0)),
            pl.BlockSpec((tm, ROUTE_W), lambda i: (i, 0)),
        ],
        out_shape=[
            jax.ShapeDtypeStruct((T, D), F32),
            jax.ShapeDtypeStruct((T, ROW_TILES, LANE), F32),
            jax.ShapeDtypeStruct((T, ROUTE_W), jnp.int32),
            jax.ShapeDtypeStruct((T, ROUTE_W), F32),
        ],
        compiler_params=_params(("arbitrary",)),
        name="merge",
    )(oa, ob, z, z, x2, wa, wb, wo, g2, wr, br)


def _row_copy(src_hbm, row, dst_ref, slot, sem):
    return pltpu.make_async_copy(src_hbm.at[row], dst_ref.at[slot], sem)


def _dispatch_kernel(tok_ref, src_hbm, out_ref, sem):
    base = pl.program_id(0) * MOE_BM

    def issue(r, carry):
        _row_copy(src_hbm, tok_ref[base + r], out_ref, r, sem).start()
        return carry

    lax.fori_loop(0, MOE_BM, issue, 0, unroll=8)

    def drain(r, carry):
        _row_copy(src_hbm, 0, out_ref, r, sem).wait()
        return carry

    lax.fori_loop(0, MOE_BM, drain, 0, unroll=8)


def _dispatch(slot_tok, xt3, nb):
    return pl.pallas_call(
        _dispatch_kernel,
        grid_spec=pltpu.PrefetchScalarGridSpec(
            num_scalar_prefetch=1,
            grid=(nb,),
            in_specs=[pl.BlockSpec(memory_space=pl.ANY)],
            out_specs=pl.BlockSpec((MOE_BM, ROW_TILES, LANE), lambda i, tok: (i, 0, 0)),
            scratch_shapes=[pltpu.SemaphoreType.DMA(())],
        ),
        out_shape=jax.ShapeDtypeStruct((nb * MOE_BM, ROW_TILES, LANE), F32),
        compiler_params=_params(("arbitrary",)),
        name="dispatch",
    )(slot_tok, xt3)


def _ffn_kernel(be_ref, xb_ref, w1_ref, w3_ref, w2_ref, y_ref, xs_ref):
    for j in range(ROW_TILES):
        xs_ref[:, j * LANE:(j + 1) * LANE] = xb_ref[:, j, :].astype(BF16)
    xs = xs_ref[...]
    h1 = jnp.dot(xs, w1_ref[0], preferred_element_type=F32)
    h3 = jnp.dot(xs, w3_ref[0], preferred_element_type=F32)
    hh = (h1 * jax.nn.sigmoid(h1) * h3).astype(BF16)
    y = jnp.dot(hh, w2_ref[0], preferred_element_type=F32)
    for j in range(ROW_TILES):
        y_ref[:, j, :] = y[:, j * LANE:(j + 1) * LANE]


def _ffn(block_e, xb, w1, w3, w2):
    nb = xb.shape[0] // MOE_BM
    D = D_MODEL
    return pl.pallas_call(
        _ffn_kernel,
        grid_spec=pltpu.PrefetchScalarGridSpec(
            num_scalar_prefetch=1,
            grid=(nb,),
            in_specs=[
                pl.BlockSpec((MOE_BM, ROW_TILES, LANE), lambda i, be: (i, 0, 0)),
                pl.BlockSpec((1, D, EXPERT_FF), lambda i, be: (be[i], 0, 0)),
                pl.BlockSpec((1, D, EXPERT_FF), lambda i, be: (be[i], 0, 0)),
                pl.BlockSpec((1, EXPERT_FF, D), lambda i, be: (be[i], 0, 0)),
            ],
            out_specs=pl.BlockSpec((MOE_BM, ROW_TILES, LANE), lambda i, be: (i, 0, 0)),
            scratch_shapes=[pltpu.VMEM((MOE_BM, D), BF16)],
        ),
        out_shape=jax.ShapeDtypeStruct(xb.shape, F32),
        compiler_params=_params(("arbitrary",)),
        name="ffn",
    )(block_e, xb, w1, w3, w2)


def _combine_kernel(dest_ref, h_ref, gate_ref, fg_ref, y_hbm, o_ref, y0_ref, y1_ref, sem):
    base = pl.program_id(0) * COMB_TM * TOP_K
    bufs = (y0_ref, y1_ref)

    def issue(r, carry):
        for k in range(TOP_K):
            _row_copy(y_hbm, dest_ref[base + TOP_K * r + k], bufs[k], r, sem).start()
        return carry

    lax.fori_loop(0, COMB_TM, issue, 0, unroll=8)

    def drain(r, carry):
        for k in range(TOP_K):
            _row_copy(y_hbm, 0, bufs[k], r, sem).wait()
        return carry

    lax.fori_loop(0, COMB_TM, drain, 0, unroll=8)

    gates = gate_ref[...]
    g0 = gates[:, 0:1]
    g1 = gates[:, 1:2]
    m = jnp.concatenate([g0 * y0_ref[:, j, :] + g1 * y1_ref[:, j, :] for j in range(ROW_TILES)], axis=-1)
    x = h_ref[...] + m
    ms = jnp.mean(x * x, axis=-1, keepdims=True)
    o_ref[...] = x * lax.rsqrt(ms + EPS) * fg_ref[...]


def _combine(dest, h, gates, fg, y):
    T, D = h.shape
    tm = COMB_TM
    return pl.pallas_call(
        _combine_kernel,
        grid_spec=pltpu.PrefetchScalarGridSpec(
            num_scalar_prefetch=1,
            grid=(T // tm,),
            in_specs=[
                pl.BlockSpec((tm, D), lambda i, d: (i, 0)),
                pl.BlockSpec((tm, ROUTE_W), lambda i, d: (i, 0)),
                pl.BlockSpec((1, D), lambda i, d: (0, 0)),
                pl.BlockSpec(memory_space=pl.ANY),
            ],
            out_specs=pl.BlockSpec((tm, D), lambda i, d: (i, 0)),
            scratch_shapes=[
                pltpu.VMEM((tm, ROW_TILES, LANE), F32),
                pltpu.VMEM((tm, ROW_TILES, LANE), F32),
                pltpu.SemaphoreType.DMA(()),
            ],
        ),
        out_shape=jax.ShapeDtypeStruct((T, D), F32),
        compiler_params=_params(("arbitrary",)),
        name="combine",
    )(dest, h, gates, fg, y)


def _route_plan(eid, T):
    A = T * TOP_K
    nb = A // MOE_BM + N_EXPERTS
    e_flat = eid[:, :TOP_K].reshape(A)
    onehot = (e_flat[:, None] == jnp.arange(N_EXPERTS, dtype=jnp.int32)[None, :]).astype(jnp.int32)
    csum = jnp.cumsum(onehot, axis=0)
    rank = jnp.sum(onehot * csum, axis=1) - 1
    counts = csum[-1]
    nb_e = (counts + MOE_BM - 1) // MOE_BM
    bend = jnp.cumsum(nb_e)
    bstart = bend - nb_e
    dest = (bstart[e_flat] * MOE_BM + rank).astype(jnp.int32)
    block_e = jnp.minimum(jnp.searchsorted(bend, jnp.arange(nb, dtype=jnp.int32), side="right"),
                          N_EXPERTS - 1).astype(jnp.int32)
    slot_tok = jnp.zeros((nb * MOE_BM,), jnp.int32).at[dest].set(jnp.arange(A, dtype=jnp.int32) // TOP_K)
    return dest, block_e, slot_tok, nb


def _encode(x, p):
    B, S, D = x.shape
    T = B * S
    x2 = x.reshape(T, D)
    z = _proj(x2, p["norm1"], p["w_in"])
    oa = _diff_attention(z, p["da_bias"], p["lams"], p["subln"], B, S)
    ob = _neighborhood_attention(z, p["na_bias"](S // GRID_W), B, S)
    h, xt3, eid, gates = _merge(oa, ob, z, x2, p["wa"], p["wb"], p["wo"], p["norm2"], p["wr"], p["br"])
    dest, block_e, slot_tok, nb = _route_plan(eid, T)
    xb = _dispatch(slot_tok, xt3, nb)
    y = _ffn(block_e, xb, p["w1"], p["w3"], p["w2"])
    out = _combine(dest, h, gates, p["final_norm"], y)
    return out.reshape(B, S, D)


def _prepare(rel_bias, norm1, w_in, lam_q1, lam_k1, lam_q2, lam_k2, subln_g, na_rpb, w_a_proj, w_b_proj, w_o,
             norm2, w_route_group, b_route_group, w_route_expert, b_route_expert, w1, w3, w2, final_norm):
    D = D_MODEL
    wr = jnp.concatenate([w_route_group[0], jnp.transpose(w_route_expert[0], (1, 0, 2)).reshape(D, N_EXPERTS)], axis=1)
    wr = jnp.pad(wr, ((0, 0), (0, ROUTE_W - wr.shape[1]))).astype(BF16)
    br = jnp.concatenate([b_route_group[0], b_route_expert[0].reshape(N_EXPERTS)])
    br = jnp.pad(br, (0, ROUTE_W - br.shape[0])).reshape(1, ROUTE_W).astype(F32)
    na_cache = {}

    def na_bias(rows):
        if rows not in na_cache:
            na_cache[rows] = _na_bias_patterns(na_rpb[0], rows)
        return na_cache[rows]

    return {
        "norm1": norm1[0].reshape(1, D).astype(F32),
        "w_in": w_in[0].astype(BF16),
        "lams": tuple(v[0].reshape(1, DA_DK).astype(F32) for v in (lam_q1, lam_k1, lam_q2, lam_k2)),
        "subln": subln_g[0].reshape(DA_DV, 1).astype(F32),
        "da_bias": _da_bias_patterns(rel_bias),
        "na_bias": na_bias,
        "wa": w_a_proj[0].astype(BF16),
        "wb": w_b_proj[0].astype(BF16),
        "wo": w_o[0].astype(BF16),
        "norm2": norm2[0].reshape(1, D).astype(F32),
        "wr": wr,
        "br": br,
        "w1": w1[0].astype(BF16),
        "w3": w3[0].astype(BF16),
        "w2": w2[0].astype(BF16),
        "final_norm": final_norm.reshape(1, D).astype(F32),
    }


def kernel(x_prompt, x_sample, rel_bias, norm1, w_in, lam_q1, lam_k1, lam_q2, lam_k2, subln_g, na_rpb, w_a_proj,
           w_b_proj, w_o, norm2, w_route_group, b_route_group, w_route_expert, b_route_expert, w1, w3, w2,
           final_norm):
    p = _prepare(rel_bias, norm1, w_in, lam_q1, lam_k1, lam_q2, lam_k2, subln_g, na_rpb, w_a_proj, w_b_proj, w_o,
                 norm2, w_route_group, b_route_group, w_route_expert, b_route_expert, w1, w3, w2, final_norm)
    return (_encode(x_prompt, p), _encode(x_sample, p))
```
